```python
import jax, jax.numpy as jnp
from jax import lax
import numpy as np

D_MODEL = 2048
BATCH = 1
SEQ = 8192
DEPTH = 2

GRID_W = 64
CTX_LEN = 256
HEAD_DIM = 128
N_BRANCH = 4
BRANCH_W = D_MODEL // N_BRANCH
A_HEADS = BRANCH_W // HEAD_DIM
A_KV_HEADS = A_HEADS // 2
KV_W = A_KV_HEADS * HEAD_DIM
ROPE_THETA = 10000.0
ROPE_HALF = HEAD_DIM // 4
Q_BLOCK = 128
B_GROUPS = 4
B_GROUP_W = BRANCH_W // B_GROUPS
POOL_WINDOWS = (2, 4, 8, 16)
C_GROUPS = 4
C_GROUP_W = BRANCH_W // C_GROUPS
C_CHUNK = 128
D_HEADS = BRANCH_W // HEAD_DIM
NA_WIN_H = 8
NA_WIN_W = 16
FFN_HIDDEN = -(-(8 * D_MODEL) // (3 * 256)) * 256
EPS = 1e-6

A_Q0 = 0
A_K0 = A_Q0 + A_HEADS * HEAD_DIM
A_V0 = A_K0 + KV_W
A_END = A_V0 + KV_W
B0 = A_END
B_END = B0 + BRANCH_W
C_U0 = B_END
C_V0 = C_U0 + BRANCH_W
C_END = C_V0 + BRANCH_W
D_Q0 = C_END
D_K0 = D_Q0 + D_HEADS * HEAD_DIM
D_V0 = D_K0 + D_HEADS * HEAD_DIM
D_END = D_V0 + D_HEADS * HEAD_DIM
G0 = D_END
IN_COLS = G0 + N_BRANCH * D_MODEL

kernel_name = 'hybrid_dit_parallel_gated_mixers'


def _rms(x, g):
    xf = x.astype(jnp.float32)
    y = xf * lax.rsqrt(jnp.mean(xf * xf, axis=-1, keepdims=True) + EPS)
    return (y * g.astype(jnp.float32)).astype(x.dtype)


def _layernorm(x, g, b):
    xf = x.astype(jnp.float32)
    mu = jnp.mean(xf, axis=-1, keepdims=True)
    var = jnp.mean(jnp.square(xf - mu), axis=-1, keepdims=True)
    y = (xf - mu) * lax.rsqrt(var + EPS) * g.astype(jnp.float32) + b.astype(jnp.float32)
    return y.astype(x.dtype)


def _modulate(h, shift, scale):
    return h * (1.0 + scale) + shift


def _heads(a):
    return a.reshape(a.shape[:-1] + (-1, HEAD_DIM))


def _rope_tables(n):
    t = jnp.arange(n)
    row = (t // GRID_W).astype(jnp.float32)
    col = (t % GRID_W).astype(jnp.float32)
    inv = ROPE_THETA ** (-jnp.arange(ROPE_HALF, dtype=jnp.float32) / ROPE_HALF)
    ang = jnp.concatenate([row[:, None] * inv, col[:, None] * inv], axis=-1)
    return jnp.cos(ang), jnp.sin(ang)


def _apply_rope(x, cos, sin):
    b, n, h, d = x.shape
    xr = x.reshape(b, n, h, 2, 2, ROPE_HALF)
    x1, x2 = xr[..., 0, :], xr[..., 1, :]
    cs = cos.reshape(n, 2, ROPE_HALF)[None, :, None].astype(x.dtype)
    sn = sin.reshape(n, 2, ROPE_HALF)[None, :, None].astype(x.dtype)
    return jnp.stack([x1 * cs - x2 * sn, x2 * cs + x1 * sn], axis=-2).reshape(b, n, h, d)


def _block_attention(q, k, v):
    b, n, h, d = q.shape
    hkv = k.shape[2]
    g = h // hkv
    qb = q.reshape(b, n // Q_BLOCK, Q_BLOCK, hkv, g, d).transpose(1, 0, 2, 3, 4, 5)
    scale = d ** -0.5

    def one(qblk):
        s = jnp.einsum('bqkgd,bskd->bkgqs', qblk, k).astype(jnp.float32) * scale
        p = jax.nn.softmax(s, axis=-1).astype(v.dtype)
        return jnp.einsum('bkgqs,bskd->bqkgd', p, v)

    o = lax.map(one, qb)
    return o.transpose(1, 0, 2, 3, 4, 5).reshape(b, n, h * d)


def _pool_mix(p, w, scale):
    b, n, _ = p.shape
    pg = p.reshape(b, n, B_GROUPS, B_GROUP_W).astype(jnp.float32)
    cs = jnp.concatenate([jnp.zeros_like(pg[:, :1]), jnp.cumsum(pg, axis=1)], axis=1)
    t = jnp.arange(n)[:, None]
    half = jnp.array(POOL_WINDOWS, jnp.int32)[None, :] // 2
    lo = jnp.clip(t - half, 0, n)
    hi = jnp.clip(t + half, 0, n)
    gidx = jnp.arange(B_GROUPS)[None, :]
    win = cs[:, hi, gidx] - cs[:, lo, gidx]
    mean = win / (hi - lo).astype(jnp.float32)[None, :, :, None]
    dlt = (mean - pg).astype(p.dtype)
    y = jnp.einsum('bngc,gce->bnge', dlt, w).reshape(b, n, BRANCH_W)
    return y * scale


def _chunk_mlp(u, v, g, bn, ws, bs):
    b, n, _ = v.shape
    vn = _layernorm(v, g, bn).reshape(b, n // C_CHUNK, C_CHUNK, C_GROUPS, C_GROUP_W)
    mixed = jnp.einsum('gpq,bkqgc->bkpgc', ws, vn) + bs.T[None, None, :, :, None]
    return u * mixed.reshape(b, n, BRANCH_W)


def _neighbourhood_attention(q, k, v, k_ctx, v_ctx, rpb, rows):
    b, n, h, d = q.shape
    wh = min(NA_WIN_H, rows)
    qg = q.reshape(b, rows, GRID_W, h, d)
    kg = k.reshape(b, rows, GRID_W, h, d)
    vg = v.reshape(b, rows, GRID_W, h, d)
    r = jnp.arange(rows)
    r0 = jnp.clip(r - wh // 2, 0, rows - wh)
    dr_idx = r0[:, None] + jnp.arange(wh)[None, :] - r[:, None] + (NA_WIN_H - 1)
    cc = jnp.arange(GRID_W)
    c0 = jnp.clip(cc - NA_WIN_W // 2, 0, GRID_W - NA_WIN_W)
    col_idx = c0[:, None] + jnp.arange(NA_WIN_W)[None, :]
    dc_idx = col_idx - cc[:, None] + (NA_WIN_W - 1)
    rpb_cols = rpb[:, :, dc_idx]
    scale = d ** -0.5
    n_loc = wh * NA_WIN_W

    def one_row(args):
        q_row, start, dr = args
        k_rows = lax.dynamic_slice_in_dim(kg, start, wh, axis=1)
        v_rows = lax.dynamic_slice_in_dim(vg, start, wh, axis=1)
        k_win = k_rows[:, :, col_idx]
        v_win = v_rows[:, :, col_idx]
        bias = rpb_cols[:, dr].transpose(0, 2, 1, 3).astype(jnp.float32)
        s_loc = jnp.einsum('bchd,bicjhd->bhcij', q_row, k_win).astype(jnp.float32) * scale + bias
        s_ctx = jnp.einsum('bchd,blhd->bhcl', q_row, k_ctx).astype(jnp.float32) * scale
        s = jnp.concatenate([s_loc.reshape(b, h, GRID_W, n_loc), s_ctx], axis=-1)
        p = jax.nn.softmax(s, axis=-1).astype(v.dtype)
        p_loc = p[..., :n_loc].reshape(b, h, GRID_W, wh, NA_WIN_W)
        p_ctx = p[..., n_loc:]
        return (jnp.einsum('bhcij,bicjhd->bchd', p_loc, v_win)
                + jnp.einsum('bhcl,blhd->bchd', p_ctx, v_ctx))

    o = lax.map(one_row, (qg.transpose(1, 0, 2, 3, 4), r0, dr_idx))
    return o.transpose(1, 0, 2, 3, 4).reshape(b, n, h * d)


def _merge(outs, gate_logits, w_br, w_o):
    y = jnp.einsum('bnik,ikd->bnid', outs, w_br)
    gates = jax.nn.sigmoid(gate_logits.reshape(y.shape))
    return jnp.sum(gates * y, axis=2) @ w_o


def _swiglu(h, wg, wu, wd):
    return (jax.nn.silu(h @ wg) * (h @ wu)) @ wd


def setup_inputs(seed: int = 0) -> dict:
    key = jax.random.key(seed)
    ks = jax.random.split(key, 25)
    L, D = DEPTH, D_MODEL

    def nrm(k, shape, s):
        return jax.random.normal(k, shape, jnp.float32) * s

    def gain(k, shape):
        return 1.0 + 0.02 * jax.random.normal(k, shape, jnp.float32)

    return {
        'x': nrm(ks[0], (BATCH, SEQ, D), 1.0),
        'c': nrm(ks[1], (BATCH, D), 1.0),
        'ctx': nrm(ks[2], (BATCH, CTX_LEN, D), 1.0),
        'c_ctx': nrm(ks[3], (D,), 1.0),
        'ada_w': nrm(ks[4], (L, D, 6 * D), 0.5 * D ** -0.5),
        'ada_b': nrm(ks[5], (L, 6 * D), 0.01),
        'norm_pre_mix': gain(ks[6], (L, D)),
        'norm_post_mix': gain(ks[7], (L, D)),
        'norm_pre_ffn': gain(ks[8], (L, D)),
        'norm_post_ffn': gain(ks[9], (L, D)),
        'w_in': nrm(ks[10], (L, D, IN_COLS), D ** -0.5),
        'a_q_norm': gain(ks[11], (L, HEAD_DIM)),
        'a_k_norm': gain(ks[12], (L, HEAD_DIM)),
        'b_w': nrm(ks[13], (L, B_GROUPS, B_GROUP_W, B_GROUP_W), B_GROUP_W ** -0.5),
        'b_scale': gain(ks[14], (L, BRANCH_W)),
        'c_norm_g': gain(ks[15], (L, BRANCH_W)),
        'c_norm_b': nrm(ks[16], (L, BRANCH_W), 0.02),
        'c_ws': nrm(ks[17], (L, C_GROUPS, C_CHUNK, C_CHUNK), C_CHUNK ** -0.5),
        'c_bs': gain(ks[18], (L, C_GROUPS, C_CHUNK)),
        'd_rpb': nrm(ks[19], (L, D_HEADS, 2 * NA_WIN_H - 1, 2 * NA_WIN_W - 1), 0.1),
        'w_br': nrm(ks[20], (L, N_BRANCH, BRANCH_W, D), BRANCH_W ** -0.5),
        'w_o': nrm(ks[21], (L, D, D), D ** -0.5),
        'w_gate': nrm(ks[22], (L, D, FFN_HIDDEN), D ** -0.5),
        'w_up': nrm(ks[23], (L, D, FFN_HIDDEN), D ** -0.5),
        'w_down': nrm(ks[24], (L, FFN_HIDDEN, D), FFN_HIDDEN ** -0.5),
    }


def reference(x, c, ctx, c_ctx, ada_w, ada_b, norm_pre_mix, norm_post_mix, norm_pre_ffn,
              norm_post_ffn, w_in, a_q_norm, a_k_norm, b_w, b_scale, c_norm_g, c_norm_b,
              c_ws, c_bs, d_rpb, w_br, w_o, w_gate, w_up, w_down):
    n = x.shape[1]
    rows = n // GRID_W
    cos, sin = _rope_tables(n)
    xl, xc = x, ctx
    for layer in range(DEPTH):
        last = layer == DEPTH - 1
        w = w_in[layer]
        mod_l = (jax.nn.silu(c) @ ada_w[layer] + ada_b[layer])[:, None, :]
        mod_c = (jax.nn.silu(c_ctx) @ ada_w[layer] + ada_b[layer])[None, None, :]
        sh_l, sc_l, g_l, shf_l, scf_l, gf_l = jnp.split(mod_l, 6, axis=-1)
        sh_c, sc_c, g_c, shf_c, scf_c, gf_c = jnp.split(mod_c, 6, axis=-1)

        hl = _modulate(_rms(xl, norm_pre_mix[layer]), sh_l, sc_l)
        hc = _modulate(_rms(xc, norm_pre_mix[layer]), sh_c, sc_c)

        if last:
            pc_a = hc @ w[:, A_K0:A_END]
            pc_d = hc @ w[:, D_K0:D_END]
        else:
            pc = hc @ w
            pc_a = pc[..., A_K0:A_END]
            pc_d = pc[..., D_K0:D_END]
        ka_c = _rms(_heads(pc_a[..., :KV_W]), a_k_norm[layer])
        va_c = _heads(pc_a[..., KV_W:])
        kd_c = _heads(pc_d[..., :BRANCH_W])
        vd_c = _heads(pc_d[..., BRANCH_W:])

        pl = hl @ w
        qa = _apply_rope(_rms(_heads(pl[..., A_Q0:A_K0]), a_q_norm[layer]), cos, sin)
        ka = _apply_rope(_rms(_heads(pl[..., A_K0:A_V0]), a_k_norm[layer]), cos, sin)
        va = _heads(pl[..., A_V0:A_END])
        out_a = _block_attention(qa, jnp.concatenate([ka, ka_c], axis=1),
                                 jnp.concatenate([va, va_c], axis=1))
        out_b = _pool_mix(pl[..., B0:B_END], b_w[layer], b_scale[layer])
        out_c = _chunk_mlp(pl[..., C_U0:C_V0], pl[..., C_V0:C_END], c_norm_g[layer],
                           c_norm_b[layer], c_ws[layer], c_bs[layer])
        out_d = _neighbourhood_attention(_heads(pl[..., D_Q0:D_K0]), _heads(pl[..., D_K0:D_V0]),
                                         _heads(pl[..., D_V0:D_END]), kd_c, vd_c, d_rpb[layer], rows)
        y_l = _merge(jnp.stack([out_a, out_b, out_c, out_d], axis=2), pl[..., G0:],
                     w_br[layer], w_o[layer])
        xl = xl + g_l * _rms(y_l, norm_post_mix[layer])
        hf_l = _modulate(_rms(xl, norm_pre_ffn[layer]), shf_l, scf_l)
        xl = xl + gf_l * _rms(_swiglu(hf_l, w_gate[layer], w_up[layer], w_down[layer]),
                              norm_post_ffn[layer])

        if not last:
            qa_c = _rms(_heads(pc[..., A_Q0:A_K0]), a_q_norm[layer])
            oa_c = _block_attention(qa_c, ka_c, va_c)
            ob_c = _pool_mix(pc[..., B0:B_END], b_w[layer], b_scale[layer])
            oc_c = _chunk_mlp(pc[..., C_U0:C_V0], pc[..., C_V0:C_END], c_norm_g[layer],
                              c_norm_b[layer], c_ws[layer], c_bs[layer])
            od_c = _block_attention(_heads(pc[..., D_Q0:D_K0]), kd_c, vd_c)
            y_c = _merge(jnp.stack([oa_c, ob_c, oc_c, od_c], axis=2), pc[..., G0:],
                         w_br[layer], w_o[layer])
            xc = xc + g_c * _rms(y_c, norm_post_mix[layer])
            hf_c = _modulate(_rms(xc, norm_pre_ffn[layer]), shf_c, scf_c)
            xc = xc + gf_c * _rms(_swiglu(hf_c, w_gate[layer], w_up[layer], w_down[layer]),
                                  norm_post_ffn[layer])
    return xl
```

```python
import functools

import numpy as np
import jax
import jax.numpy as jnp
from jax import lax
from jax.experimental import pallas as pl
from jax.experimental.pallas import tpu as pltpu

F32 = jnp.float32
BF16 = jnp.bfloat16

D_MODEL = 2048
DEPTH = 2
GRID_W = 64
HEAD_DIM = 128
BRANCH_W = 512
A_HEADS = 4
A_KV_HEADS = 2
KV_W = A_KV_HEADS * HEAD_DIM
ROPE_THETA = 10000.0
ROPE_HALF = HEAD_DIM // 4
B_GROUPS = 4
POOL_WINDOWS = (2, 4, 8, 16)
POOL_HALO = 8
C_GROUPS = 4
C_CHUNK = 128
D_HEADS = 4
NA_WIN_H = 8
NA_WIN_W = 16
FFN_HIDDEN = 5632
EPS = 1e-6

A_Q0 = 0
A_K0 = 512
A_END = 1024
B0 = 1024
C_U0 = 1536
C_END = 2560
D_Q0 = 2560
D_K0 = 3072
D_END = 4096
G0 = 4096

NA_Q_ROWS = 8
NA_K_ROWS = 16
NA_KBLK = 4 * GRID_W
MASK_NEG = -1e30
ATT_SCALE = HEAD_DIM ** -0.5
V7X_VMEM_LIMIT = 56 * 1024 * 1024


def _params(sem, vmem=None):
    return pltpu.CompilerParams(dimension_semantics=sem, vmem_limit_bytes=vmem)


def _rms_rows(x, g):
    return x * lax.rsqrt(jnp.mean(x * x, axis=-1, keepdims=True) + EPS) * g


def _dot(a, b):
    return jnp.dot(a, b, preferred_element_type=F32)


def _dot_nt(a, b):
    return lax.dot_general(a, b, (((1,), (1,)), ((), ())), preferred_element_type=F32)


def _mod_kernel(c_ref, w_ref, b_ref, o_ref):
    cc = c_ref[...]
    s = (cc * jax.nn.sigmoid(cc)).astype(BF16)
    o_ref[0] = _dot(s, w_ref[0].astype(BF16)) + b_ref[0]


def _modulation(cc, ada_w, ada_b):
    L, D, N = ada_w.shape
    tn = 1024
    return pl.pallas_call(
        _mod_kernel,
        grid=(L, N // tn),
        in_specs=[
            pl.BlockSpec((8, D), lambda l, j: (0, 0)),
            pl.BlockSpec((1, D, tn), lambda l, j: (l, 0, j)),
            pl.BlockSpec((1, 1, tn), lambda l, j: (l, 0, j)),
        ],
        out_specs=pl.BlockSpec((1, 8, tn), lambda l, j: (l, 0, j)),
        out_shape=jax.ShapeDtypeStruct((L, 8, N), F32),
        compiler_params=_params(("parallel", "parallel"), 40 * 2**20),
        name="adaln_mod",
    )(cc, ada_w, ada_b.reshape(L, 1, N))


def _prenorm_kernel(x_ref, g_ref, sh_ref, sc_ref, o_ref):
    y = _rms_rows(x_ref[...], g_ref[...])
    o_ref[...] = (y * (1.0 + sc_ref[...]) + sh_ref[...]).astype(BF16)


def _prenorm(x, g, shift, scale, tm):
    M, D = x.shape
    vec = pl.BlockSpec((1, D), lambda i: (0, 0))
    return pl.pallas_call(
        _prenorm_kernel,
        grid=(M // tm,),
        in_specs=[pl.BlockSpec((tm, D), lambda i: (i, 0)), vec, vec, vec],
        out_specs=pl.BlockSpec((tm, D), lambda i: (i, 0)),
        out_shape=jax.ShapeDtypeStruct((M, D), BF16),
        compiler_params=_params(("parallel",)),
        name="prenorm",
    )(x, g, shift, scale)


def _a_proj_kernel(has_q, rope, *refs):
    it = iter(refs)
    h_ref, w_ref, gq_ref, gk_ref = next(it), next(it), next(it), next(it)
    if rope:
        c_ref, sa_ref, sb_ref = next(it), next(it), next(it)
    q_out = next(it) if has_q else None
    k_out, v_out = next(it), next(it)

    def norm_rope(x, g):
        y = _rms_rows(x, g)
        if rope:
            y = (y * c_ref[...] + pltpu.roll(y, 96, 1) * sa_ref[...]
                 + pltpu.roll(y, 32, 1) * sb_ref[...])
        return y.astype(BF16)

    acc = _dot(h_ref[...], w_ref[...])
    col = 0
    if has_q:
        for hd in range(A_HEADS):
            q_out[:, hd * HEAD_DIM:(hd + 1) * HEAD_DIM] = norm_rope(
                acc[:, col:col + HEAD_DIM], gq_ref[...])
            col += HEAD_DIM
    for hd in range(A_KV_HEADS):
        k_out[:, hd * HEAD_DIM:(hd + 1) * HEAD_DIM] = norm_rope(
            acc[:, col:col + HEAD_DIM], gk_ref[...])
        col += HEAD_DIM
    v_out[...] = acc[:, col:col + KV_W].astype(BF16)


def _a_proj(h, w, gq, gk, rope_tabs, has_q, tm):
    M, D = h.shape
    N = w.shape[1]
    rope = rope_tabs is not None
    row = lambda width: pl.BlockSpec((tm, width), lambda i: (i, 0))
    vec = pl.BlockSpec((1, HEAD_DIM), lambda i: (0, 0))
    in_specs = [row(D), pl.BlockSpec((D, N), lambda i: (0, 0)), vec, vec]
    args = [h, w, gq, gk]
    if rope:
        in_specs += [row(HEAD_DIM)] * 3
        args += list(rope_tabs)
    out_specs, out_shape = [], []
    if has_q:
        out_specs.append(row(BRANCH_W))
        out_shape.append(jax.ShapeDtypeStruct((M, BRANCH_W), BF16))
    out_specs += [row(KV_W), row(KV_W)]
    out_shape += [jax.ShapeDtypeStruct((M, KV_W), BF16)] * 2
    return pl.pallas_call(
        functools.partial(_a_proj_kernel, has_q, rope),
        grid=(M // tm,),
        in_specs=in_specs,
        out_specs=out_specs,
        out_shape=out_shape,
        compiler_params=_params(("parallel",), 40 * 2**20),
        name="proj_a",
    )(*args)


def _bc_proj_kernel(h_ref, w_ref, lng_ref, lnb_ref, b_out, u_out, vn_out):
    acc = _dot(h_ref[...], w_ref[...])
    b_out[...] = acc[:, 0:BRANCH_W]
    u_out[...] = acc[:, BRANCH_W:2 * BRANCH_W]
    v = acc[:, 2 * BRANCH_W:3 * BRANCH_W]
    mu = jnp.mean(v, axis=-1, keepdims=True)
    vc = v - mu
    var = jnp.mean(vc * vc, axis=-1, keepdims=True)
    vn_out[...] = (vc * lax.rsqrt(var + EPS) * lng_ref[...] + lnb_ref[...]).astype(BF16)


def _bc_proj(h, w, ln_g, ln_b, tm):
    M, D = h.shape
    N = w.shape[1]
    row = lambda width: pl.BlockSpec((tm, width), lambda i: (i, 0))
    vec = pl.BlockSpec((1, BRANCH_W), lambda i: (0, 0))
    return pl.pallas_call(
        _bc_proj_kernel,
        grid=(M // tm,),
        in_specs=[row(D), pl.BlockSpec((D, N), lambda i: (0, 0)), vec, vec],
        out_specs=[row(BRANCH_W)] * 3,
        out_shape=[jax.ShapeDtypeStruct((M, BRANCH_W), F32),
                   jax.ShapeDtypeStruct((M, BRANCH_W), F32),
                   jax.ShapeDtypeStruct((M, BRANCH_W), BF16)],
        compiler_params=_params(("parallel",), 40 * 2**20),
        name="proj_bc",
    )(h, w, ln_g, ln_b)


def _plain_proj_kernel(h_ref, w_ref, o_ref):
    o_ref[...] = _dot(h_ref[...], w_ref[...]).astype(BF16)


def _plain_proj(h, w, tm):
    M, D = h.shape
    N = w.shape[1]
    return pl.pallas_call(
        _plain_proj_kernel,
        grid=(M // tm,),
        in_specs=[pl.BlockSpec((tm, D), lambda i: (i, 0)),
                  pl.BlockSpec((D, N), lambda i: (0, 0))],
        out_specs=pl.BlockSpec((tm, N), lambda i: (i, 0)),
        out_shape=jax.ShapeDtypeStruct((M, N), BF16),
        compiler_params=_params(("parallel",), 40 * 2**20),
        name="proj_d",
    )(h, w)


def _flash_kernel(group, q_ref, k_ref, v_ref, o_ref, m_ref, l_ref, acc_ref):
    j = pl.program_id(2)

    @pl.when(j == 0)
    def _():
        m_ref[...] = jnp.full(m_ref.shape, -jnp.inf, F32)
        l_ref[...] = jnp.zeros(l_ref.shape, F32)
        acc_ref[...] = jnp.zeros(acc_ref.shape, F32)

    k = k_ref[...]
    v = v_ref[...]
    for g in range(group):
        q = q_ref[:, g * HEAD_DIM:(g + 1) * HEAD_DIM]
        s = _dot_nt(q, k) * ATT_SCALE
        m_prev = m_ref[g]
        m_new = jnp.maximum(m_prev, jnp.max(s, axis=-1, keepdims=True))
        alpha = jnp.exp(m_prev - m_new)
        p = jnp.exp(s - m_new)
        l_ref[g] = alpha * l_ref[g] + jnp.sum(p, axis=-1, keepdims=True)
        acc_ref[g] = alpha * acc_ref[g] + _dot(p.astype(BF16), v)
        m_ref[g] = m_new

    @pl.when(j == pl.num_programs(2) - 1)
    def _():
        for g in range(group):
            o_ref[:, g * HEAD_DIM:(g + 1) * HEAD_DIM] = (acc_ref[g] / l_ref[g]).astype(BF16)


def _flash_attention(q, k, v, n_kv_heads, tq, tk):
    Mq, Mk = q.shape[0], k.shape[0]
    group = q.shape[1] // HEAD_DIM // n_kv_heads
    gw = group * HEAD_DIM
    return pl.pallas_call(
        functools.partial(_flash_kernel, group),
        grid=(n_kv_heads, Mq // tq, Mk // tk),
        in_specs=[pl.BlockSpec((tq, gw), lambda h, i, j: (i, h)),
                  pl.BlockSpec((tk, HEAD_DIM), lambda h, i, j: (j, h)),
                  pl.BlockSpec((tk, HEAD_DIM), lambda h, i, j: (j, h))],
        out_specs=pl.BlockSpec((tq, gw), lambda h, i, j: (i, h)),
        out_shape=jax.ShapeDtypeStruct(q.shape, BF16),
        scratch_shapes=[pltpu.VMEM((group, tq, 1), F32),
                        pltpu.VMEM((group, tq, 1), F32),
                        pltpu.VMEM((group, tq, HEAD_DIM), F32)],
        compiler_params=_params(("parallel", "parallel", "arbitrary")),
        name="flash_attn",
    )(q, k, v)


def _na_kernel(q_ref, k0, k1, k2, k3, v0, v1, v2, v3, kc_ref, vc_ref, bias_ref, o_ref):
    q = q_ref[...]
    s_loc = [_dot_nt(q, kr[...]) * ATT_SCALE + bias_ref[0, 0, :, n * NA_KBLK:(n + 1) * NA_KBLK]
             for n, kr in enumerate((k0, k1, k2, k3))]
    s_ctx = _dot_nt(q, kc_ref[...]) * ATT_SCALE
    m = jnp.max(s_ctx, axis=-1, keepdims=True)
    for s in s_loc:
        m = jnp.maximum(m, jnp.max(s, axis=-1, keepdims=True))
    p_ctx = jnp.exp(s_ctx - m)
    l = jnp.sum(p_ctx, axis=-1, keepdims=True)
    acc = _dot(p_ctx.astype(BF16), vc_ref[...])
    for s, vr in zip(s_loc, (v0, v1, v2, v3)):
        p = jnp.exp(s - m)
        l = l + jnp.sum(p, axis=-1, keepdims=True)
        acc = acc + _dot(p.astype(BF16), vr[...])
    o_ref[...] = (acc / l).astype(BF16)


def _na_bias_tables(rpb, rows):
    nblk = rows // NA_Q_ROWS
    a = np.arange(NA_Q_ROWS)
    bp = np.arange(NA_K_ROWS)
    c = np.arange(GRID_W)
    c0 = np.clip(c - NA_WIN_W // 2, 0, GRID_W - NA_WIN_W)
    vc = (c[None, :] >= c0[:, None]) & (c[None, :] < c0[:, None] + NA_WIN_W)
    dc = np.clip(c[None, :] - c[:, None] + NA_WIN_W - 1, 0, 2 * NA_WIN_W - 2)
    tabs = []
    for bt in (0, 1, nblk - 1):
        r = NA_Q_ROWS * bt + a
        ks = int(np.clip(NA_Q_ROWS * bt - NA_WIN_H // 2, 0, rows - NA_K_ROWS))
        kr = ks + bp
        r0 = np.clip(r - NA_WIN_H // 2, 0, rows - NA_WIN_H)
        vr = (kr[None, :] >= r0[:, None]) & (kr[None, :] < r0[:, None] + NA_WIN_H)
        dr = np.clip(kr[None, :] - r[:, None] + NA_WIN_H - 1, 0, 2 * NA_WIN_H - 2)
        g = rpb[:, dr[:, :, None, None], dc[None, None, :, :]]
        valid = vr[:, :, None, None] & vc[None, None, :, :]
        g = jnp.where(valid[None], g, MASK_NEG)
        tabs.append(g.transpose(0, 1, 3, 2, 4).reshape(
            D_HEADS, NA_Q_ROWS * GRID_W, NA_K_ROWS * GRID_W))
    return jnp.stack(tabs)


def _neighbourhood_attention(qkv, kv_ctx, bias, rows):
    n = qkv.shape[0]
    nblk = rows // NA_Q_ROWS
    tq = NA_Q_ROWS * GRID_W
    L = kv_ctx.shape[0]
    last_kblk = (rows - NA_K_ROWS) * GRID_W // NA_KBLK

    def kblk(b, j):
        return jnp.clip(2 * b - 1, 0, last_kblk) + j

    def btype(b):
        return jnp.where(b == 0, 0, jnp.where(b == nblk - 1, 2, 1))

    k_specs = [pl.BlockSpec((NA_KBLK, HEAD_DIM), functools.partial(
        lambda h, b, j: (kblk(b, j), D_HEADS + h), j=j)) for j in range(4)]
    v_specs = [pl.BlockSpec((NA_KBLK, HEAD_DIM), functools.partial(
        lambda h, b, j: (kblk(b, j), 2 * D_HEADS + h), j=j)) for j in range(4)]
    return pl.pallas_call(
        _na_kernel,
        grid=(D_HEADS, nblk),
        in_specs=[pl.BlockSpec((tq, HEAD_DIM), lambda h, b: (b, h))] + k_specs + v_specs + [
            pl.BlockSpec((L, HEAD_DIM), lambda h, b: (0, h)),
            pl.BlockSpec((L, HEAD_DIM), lambda h, b: (0, D_HEADS + h)),
            pl.BlockSpec((1, 1, tq, NA_K_ROWS * GRID_W), lambda h, b: (btype(b), h, 0, 0)),
        ],
        out_specs=pl.BlockSpec((tq, HEAD_DIM), lambda h, b: (b, h)),
        out_shape=jax.ShapeDtypeStruct((n, BRANCH_W), BF16),
        compiler_params=_params(("parallel", "arbitrary"), 40 * 2**20),
        name="nbr_attn",
    )(qkv, *([qkv] * 8), kv_ctx, kv_ctx, bias)


def _pool_kernel(n_tokens, prev_ref, cur_ref, next_ref, w_ref, sc_ref, o_ref, ext_ref):
    i = pl.program_id(0)
    tm = cur_ref.shape[0]
    ext_ref[0:POOL_HALO, :] = jnp.where(i == 0, 0.0, prev_ref[...])
    ext_ref[POOL_HALO:POOL_HALO + tm, :] = cur_ref[...]
    ext_ref[POOL_HALO + tm:, :] = jnp.where(i == pl.num_programs(0) - 1, 0.0, next_ref[...])
    t = i * tm + lax.broadcasted_iota(jnp.int32, (tm, 1), 0)
    gw = BRANCH_W // B_GROUPS
    for g, win in enumerate(POOL_WINDOWS):
        half = win // 2
        cols = slice(g * gw, (g + 1) * gw)
        tot = ext_ref[POOL_HALO - half:POOL_HALO - half + tm, cols]
        for k in range(-half + 1, half):
            tot = tot + ext_ref[POOL_HALO + k:POOL_HALO + k + tm, cols]
        cnt = (jnp.minimum(t + half, n_tokens) - jnp.maximum(t - half, 0)).astype(F32)
        dlt = (tot / cnt - cur_ref[:, cols]).astype(BF16)
        o_ref[:, cols] = (_dot(dlt, w_ref[g]) * sc_ref[:, cols]).astype(BF16)


def _pool_mix(p, w, scale, tm):
    M = p.shape[0]
    nh = tm // POOL_HALO
    last = M // POOL_HALO - 1
    return pl.pallas_call(
        functools.partial(_pool_kernel, M),
        grid=(M // tm,),
        in_specs=[
            pl.BlockSpec((POOL_HALO, BRANCH_W), lambda i: (jnp.maximum(i * nh - 1, 0), 0)),
            pl.BlockSpec((tm, BRANCH_W), lambda i: (i, 0)),
            pl.BlockSpec((POOL_HALO, BRANCH_W), lambda i: (jnp.minimum((i + 1) * nh, last), 0)),
            pl.BlockSpec(w.shape, lambda i: (0, 0, 0)),
            pl.BlockSpec((1, BRANCH_W), lambda i: (0, 0)),
        ],
        out_specs=pl.BlockSpec((tm, BRANCH_W), lambda i: (i, 0)),
        out_shape=jax.ShapeDtypeStruct((M, BRANCH_W), BF16),
        scratch_shapes=[pltpu.VMEM((tm + 2 * POOL_HALO, BRANCH_W), F32)],
        compiler_params=_params(("parallel",)),
        name="pool_mix",
    )(p, p, p, w, scale)


def _chunk_mlp_kernel(u_ref, vn_ref, ws_ref, bs_ref, o_ref):
    tm = u_ref.shape[0]
    gw = BRANCH_W // C_GROUPS
    for ck in range(tm // C_CHUNK):
        rws = slice(ck * C_CHUNK, (ck + 1) * C_CHUNK)
        for g in range(C_GROUPS):
            cols = slice(g * gw, (g + 1) * gw)
            mixed = _dot(ws_ref[g], vn_ref[rws, cols]) + bs_ref[g]
            o_ref[rws, cols] = (u_ref[rws, cols] * mixed).astype(BF16)


def _chunk_mlp(u, vn, ws, bs, tm):
    M = u.shape[0]
    row = pl.BlockSpec((tm, BRANCH_W), lambda i: (i, 0))
    return pl.pallas_call(
        _chunk_mlp_kernel,
        grid=(M // tm,),
        in_specs=[row, row, pl.BlockSpec(ws.shape, lambda i: (0, 0, 0)),
                  pl.BlockSpec(bs.shape, lambda i: (0, 0, 0))],
        out_specs=row,
        out_shape=jax.ShapeDtypeStruct((M, BRANCH_W), BF16),
        compiler_params=_params(("parallel",)),
        name="chunk_mlp",
    )(u, vn, ws, bs)


def _merge_kernel(h_ref, g0, g1, g2, g3, oa, ob, oc, od, wbr_ref, o_ref):
    h = h_ref[...]
    m = None
    for br, (g_ref, x_ref) in enumerate(zip((g0, g1, g2, g3), (oa, ob, oc, od))):
        gate = jax.nn.sigmoid(_dot(h, g_ref[...]))
        term = gate * _dot(x_ref[...], wbr_ref[br])
        m = term if m is None else m + term
    o_ref[...] = m.astype(BF16)


def _merge(h, w_gate, outs, w_br, tm, tn):
    M, D = h.shape
    nj = D // tn
    g_specs = [pl.BlockSpec((D, tn), functools.partial(lambda i, j, br: (0, br * nj + j), br=br))
               for br in range(4)]
    o_spec = pl.BlockSpec((tm, BRANCH_W), lambda i, j: (i, 0))
    return pl.pallas_call(
        _merge_kernel,
        grid=(M // tm, nj),
        in_specs=[pl.BlockSpec((tm, D), lambda i, j: (i, 0))] + g_specs + [o_spec] * 4 + [
            pl.BlockSpec((4, BRANCH_W, tn), lambda i, j: (0, 0, j))],
        out_specs=pl.BlockSpec((tm, tn), lambda i, j: (i, j)),
        out_shape=jax.ShapeDtypeStruct((M, D), BF16),
        compiler_params=_params(("parallel", "arbitrary"), 48 * 2**20),
        name="gated_merge",
    )(h, *([w_gate] * 4), *outs, w_br)


def _out_proj_kernel(m_ref, w_ref, x_ref, gn_ref, gate_ref, o_ref):
    y = _dot(m_ref[...], w_ref[...])
    o_ref[...] = x_ref[...] + gate_ref[...] * _rms_rows(y, gn_ref[...])


def _out_proj(m, w_o, x, g_norm, gate, tm):
    M, D = x.shape
    row = pl.BlockSpec((tm, D), lambda i: (i, 0))
    vec = pl.BlockSpec((1, D), lambda i: (0, 0))
    return pl.pallas_call(
        _out_proj_kernel,
        grid=(M // tm,),
        in_specs=[row, pl.BlockSpec((D, D), lambda i: (0, 0)), row, vec, vec],
        out_specs=row,
        out_shape=jax.ShapeDtypeStruct((M, D), F32),
        compiler_params=_params(("parallel",), 48 * 2**20),
        name="out_proj",
    )(m, w_o, x, g_norm, gate)


def _ffn_kernel(x_ref, gpre_ref, sh_ref, sc_ref, gpost_ref, gate_ref, wg_ref, wu_ref, wd_ref,
                o_ref, h_ref, acc_ref):
    k = pl.program_id(1)

    @pl.when(k == 0)
    def _():
        y = _rms_rows(x_ref[...], gpre_ref[...])
        h_ref[...] = (y * (1.0 + sc_ref[...]) + sh_ref[...]).astype(BF16)
        acc_ref[...] = jnp.zeros(acc_ref.shape, F32)

    h = h_ref[...]
    a = _dot(h, wg_ref[...])
    b = _dot(h, wu_ref[...])
    hid = (a * jax.nn.sigmoid(a) * b).astype(BF16)
    acc_ref[...] += _dot(hid, wd_ref[...])

    @pl.when(k == pl.num_programs(1) - 1)
    def _():
        o_ref[...] = x_ref[...] + gate_ref[...] * _rms_rows(acc_ref[...], gpost_ref[...])


def _ffn(x, g_pre, shift, scale, g_post, gate, wg, wu, wd, tm, tk):
    M, D = x.shape
    H = wg.shape[1]
    row = pl.BlockSpec((tm, D), lambda i, k: (i, 0))
    vec = pl.BlockSpec((1, D), lambda i, k: (0, 0))
    return pl.pallas_call(
        _ffn_kernel,
        grid=(M // tm, H // tk),
        in_specs=[row, vec, vec, vec, vec, vec,
                  pl.BlockSpec((D, tk), lambda i, k: (0, k)),
                  pl.BlockSpec((D, tk), lambda i, k: (0, k)),
                  pl.BlockSpec((tk, D), lambda i, k: (k, 0))],
        out_specs=row,
        out_shape=jax.ShapeDtypeStruct((M, D), F32),
        scratch_shapes=[pltpu.VMEM((tm, D), BF16), pltpu.VMEM((tm, D), F32)],
        compiler_params=_params(("parallel", "arbitrary"), 48 * 2**20),
        name="swiglu_ffn",
    )(x, g_pre, shift, scale, g_post, gate, wg, wu, wd)


def _rope_tables(n):
    t = jnp.arange(n)
    row = (t // GRID_W).astype(F32)
    col = (t % GRID_W).astype(F32)
    inv = ROPE_THETA ** (-jnp.arange(ROPE_HALF, dtype=F32) / ROPE_HALF)
    cr, sr = jnp.cos(row[:, None] * inv), jnp.sin(row[:, None] * inv)
    cc, sc = jnp.cos(col[:, None] * inv), jnp.sin(col[:, None] * inv)
    z = jnp.zeros_like(sr)
    return (jnp.concatenate([cr, cr, cc, cc], axis=-1),
            jnp.concatenate([-sr, z, -sc, z], axis=-1),
            jnp.concatenate([z, sr, z, sc], axis=-1))


def kernel(x, c, ctx, c_ctx, ada_w, ada_b, norm_pre_mix, norm_post_mix, norm_pre_ffn, norm_post_ffn, w_in, a_q_norm, a_k_norm, b_w, b_scale, c_norm_g, c_norm_b, c_ws, c_bs, d_rpb, w_br, w_o, w_gate, w_up, w_down):
    assert x.shape[0] == 1 and ctx.shape[0] == 1
    n, D = x.shape[1], x.shape[2]
    n_ctx = ctx.shape[1]
    rows = n // GRID_W
    TM, TM_CTX = 512, n_ctx

    cc = jnp.zeros((8, D), F32).at[0].set(c[0]).at[1].set(c_ctx)
    mods = _modulation(cc, ada_w, ada_b)
    rope_tabs = _rope_tables(n)

    xl, xc = x[0], ctx[0]
    for layer in range(DEPTH):
        last = layer == DEPTH - 1
        mod_l = [mods[layer, 0:1, i * D:(i + 1) * D] for i in range(6)]
        mod_c = [mods[layer, 1:2, i * D:(i + 1) * D] for i in range(6)]
        vecD = lambda a: a[layer].reshape(1, -1)
        w = w_in[layer]
        w_a = w[:, A_Q0:A_END].astype(BF16)
        w_bc = w[:, B0:C_END].astype(BF16)
        w_d = w[:, D_Q0:D_END].astype(BF16)
        w_g = w[:, G0:].astype(BF16)
        gq, gk = vecD(a_q_norm), vecD(a_k_norm)
        bw = b_w[layer].astype(BF16)
        bsc = vecD(b_scale)
        lng, lnb = vecD(c_norm_g), vecD(c_norm_b)
        cws = c_ws[layer].astype(BF16)
        cbs = c_bs[layer][:, :, None]
        wbr = w_br[layer].astype(BF16)
        wo = w_o[layer].astype(BF16)
        wg, wu, wd = (w_gate[layer].astype(BF16), w_up[layer].astype(BF16),
                      w_down[layer].astype(BF16))
        npre, npost = vecD(norm_pre_mix), vecD(norm_post_mix)
        nfpre, nfpost = vecD(norm_pre_ffn), vecD(norm_post_ffn)

        hl = _prenorm(xl, npre, mod_l[0], mod_l[1], TM)
        hc = _prenorm(xc, npre, mod_c[0], mod_c[1], TM_CTX)

        if last:
            ka_c, va_c = _a_proj(hc, w_a[:, A_K0:], gq, gk, None, False, TM_CTX)
            kvd_c = _plain_proj(hc, w_d[:, BRANCH_W:], TM_CTX)
        else:
            qa_c, ka_c, va_c = _a_proj(hc, w_a, gq, gk, None, True, TM_CTX)
            qkvd_c = _plain_proj(hc, w_d, TM_CTX)
            kvd_c = qkvd_c[:, BRANCH_W:]

        qa, ka, va = _a_proj(hl, w_a, gq, gk, rope_tabs, True, TM)
        out_a = _flash_attention(qa, jnp.concatenate([ka, ka_c], axis=0),
                                 jnp.concatenate([va, va_c], axis=0), A_KV_HEADS, 512, 768)
        pb, cu, cvn = _bc_proj(hl, w_bc, lng, lnb, TM)
        out_b = _pool_mix(pb, bw, bsc, TM)
        out_c = _chunk_mlp(cu, cvn, cws, cbs, TM)
        qkvd = _plain_proj(hl, w_d, TM)
        out_d = _neighbourhood_attention(qkvd, kvd_c, _na_bias_tables(d_rpb[layer], rows), rows)
        merged = _merge(hl, w_g, (out_a, out_b, out_c, out_d), wbr, TM, 512)
        xl = _out_proj(merged, wo, xl, npost, mod_l[2], TM)
        xl = _ffn(xl, nfpre, mod_l[3], mod_l[4], nfpost, mod_l[5], wg, wu, wd, TM, 512)

        if not last:
            oa_c = _flash_attention(qa_c, ka_c, va_c, A_KV_HEADS, TM_CTX, n_ctx)
            pb_c, cu_c, cvn_c = _bc_proj(hc, w_bc, lng, lnb, TM_CTX)
            ob_c = _pool_mix(pb_c, bw, bsc, TM_CTX)
            oc_c = _chunk_mlp(cu_c, cvn_c, cws, cbs, TM_CTX)
            od_c = _flash_attention(qkvd_c[:, :BRANCH_W], kvd_c[:, :BRANCH_W],
                                    kvd_c[:, BRANCH_W:], D_HEADS, TM_CTX, n_ctx)
            merged_c = _merge(hc, w_g, (oa_c, ob_c, oc_c, od_c), wbr, TM_CTX, 512)
            xc = _out_proj(merged_c, wo, xc, npost, mod_c[2], TM_CTX)
            xc = _ffn(xc, nfpre, mod_c[3], mod_c[4], nfpost, mod_c[5], wg, wu, wd, TM_CTX, 512)
    return xl[None]
```
